```python
import math
import jax, jax.numpy as jnp
from jax import lax
import numpy as np

D_MODEL = 2048
BATCH = 8
SEQ = 2048
DEPTH = 1

EPS = 1e-6
D_ATTN = D_MODEL // 2
ATTN_HEAD_DIM = 64
N_ATTN_HEADS = D_ATTN // ATTN_HEAD_DIM
DILATED_CONFIGS = ((128, 1), (512, 4), (2048, 16))
N_BUCKETS = 32
MAX_EXACT = N_BUCKETS // 2
MAX_DISTANCE = 2048
N_GLA_HEADS = 4
D_GLA_V = D_MODEL // 2
D_GLA_K = D_GLA_V // 2
GLA_DK = D_GLA_K // N_GLA_HEADS
GLA_DV = D_GLA_V // N_GLA_HEADS
GLA_GATE_RANK = 16
GLA_GATE_TEMP = 16.0
GLA_CHUNK = 64
D_MIX = D_ATTN + D_GLA_V
IN_SPLITS = (D_ATTN, D_ATTN, D_ATTN, D_GLA_K, D_GLA_K, D_GLA_V, GLA_GATE_RANK, D_GLA_V)
D_IN = D_ATTN * 3 + D_GLA_K * 2 + D_GLA_V * 2 + GLA_GATE_RANK
PEER_HEADS = 8
PEER_N_KEYS = 128
PEER_N_EXPERTS = PEER_N_KEYS * PEER_N_KEYS
PEER_TOPK = 16
PEER_QUERY_DIM = 256
PEER_TOKEN_BLOCK = 128

kernel_name = "hybrid_dilated_gla_peer_block"


def rms_norm(x, g):
    xf = x.astype(jnp.float32)
    y = xf * lax.rsqrt(jnp.mean(xf * xf, axis=-1, keepdims=True) + EPS)
    return (y * g.astype(jnp.float32)).astype(x.dtype)


def t5_bucket(dist):
    nf = jnp.maximum(dist, 1).astype(jnp.float32)
    large = MAX_EXACT + (jnp.log(nf / MAX_EXACT) / math.log(MAX_DISTANCE / MAX_EXACT)
                         * (N_BUCKETS - MAX_EXACT)).astype(jnp.int32)
    large = jnp.minimum(large, N_BUCKETS - 1)
    return jnp.where(dist < MAX_EXACT, dist, large)


def dilated_branch(q, k, v, rel_bias, window, dilation):
    B, H, S, hd = q.shape
    d = dilation
    W = window // dilation
    L = S // d
    nb = -(-L // W)
    Lp = nb * W

    def to_blocks(t):
        t = t.reshape(B, H, L, d, hd).transpose(0, 1, 3, 2, 4)
        t = jnp.pad(t, ((0, 0), (0, 0), (0, 0), (0, Lp - L), (0, 0)))
        return t.reshape(B, H, d, nb, W, hd)

    qb, kb, vb = to_blocks(q), to_blocks(k), to_blocks(v)

    def with_prev(t):
        prev = jnp.pad(t[:, :, :, :-1], ((0, 0), (0, 0), (0, 0), (1, 0), (0, 0), (0, 0)))
        return jnp.concatenate([prev, t], axis=-2)

    kk, vv = with_prev(kb), with_prev(vb)
    a = jnp.arange(W)[:, None]
    c = jnp.arange(2 * W)[None, :]
    delta = W + a - c
    blk = jnp.arange(nb)[:, None, None]
    valid = (delta >= 0) & (delta <= W) & (blk * W + c - W >= 0)
    bucket = t5_bucket(jnp.maximum(delta, 0) * d)
    bias = rel_bias[bucket].transpose(2, 0, 1).astype(jnp.float32)

    s = jnp.einsum('bhrnqe,bhrnke->bhrnqk', qb, kk).astype(jnp.float32) * (hd ** -0.5)
    s = s + bias[None, :, None, None]
    s = jnp.where(valid[None, None, None], s, -jnp.inf)
    m = jnp.max(s, axis=-1, keepdims=True)
    p = jnp.exp(s - m)
    den = jnp.sum(p, axis=-1, keepdims=True)
    o = jnp.einsum('bhrnqk,bhrnke->bhrnqe', p, vv.astype(jnp.float32)) / den
    lse = m + jnp.log(den)

    def from_blocks(t):
        e = t.shape[-1]
        t = t.reshape(B, H, d, Lp, e)[:, :, :, :L]
        return t.transpose(0, 1, 3, 2, 4).reshape(B, H, S, e)

    return from_blocks(o), from_blocks(lse)[..., 0]


def dilated_attention(q, k, v, rel_bias):
    outs, lses = [], []
    for window, dilation in DILATED_CONFIGS:
        o, l = dilated_branch(q, k, v, rel_bias, window, dilation)
        outs.append(o)
        lses.append(l)
    wts = jax.nn.softmax(jnp.stack(lses, 0), axis=0)
    return jnp.einsum('gbhs,gbhse->bhse', wts, jnp.stack(outs, 0))


def gla_chunked(q, k, v, log_a):
    B, H, S, dk = q.shape
    dv = v.shape[-1]
    C = GLA_CHUNK
    N = S // C
    q = q.reshape(B, H, N, C, dk)
    k = k.reshape(B, H, N, C, dk)
    v = v.reshape(B, H, N, C, dv)
    b = jnp.cumsum(log_a.reshape(B, H, N, C, dk), axis=-2)
    b_last = b[..., C - 1:, :]
    b_ref = b[..., C // 2 - 1:C // 2, :]
    attn = jnp.einsum('bhnik,bhnjk->bhnij', q * jnp.exp(b - b_ref), k * jnp.exp(b_ref - b))
    causal = jnp.tril(jnp.ones((C, C), dtype=bool))
    attn = jnp.where(causal, attn, 0.0)
    o_intra = jnp.einsum('bhnij,bhnjv->bhniv', attn, v)
    kv = jnp.einsum('bhnck,bhncv->bhnkv', k * jnp.exp(b_last - b), v)
    decay = jnp.exp(b_last[..., 0, :])

    def step(state, inp):
        kv_n, dec_n = inp
        return dec_n[..., None] * state + kv_n, state

    _, states = lax.scan(step, jnp.zeros((B, H, dk, dv), jnp.float32),
                         (jnp.moveaxis(kv, 2, 0), jnp.moveaxis(decay, 2, 0)))
    states = jnp.moveaxis(states, 0, 2)
    o_inter = jnp.einsum('bhnck,bhnkv->bhncv', q * jnp.exp(b), states)
    return (o_intra + o_inter).reshape(B, H, S, dv)


def peer_ffn(xn, w_query, keys1, keys2, u, v):
    B, S, D = xn.shape
    T = B * S
    xt = xn.reshape(T, D)
    q = (xt @ w_query).reshape(T, PEER_HEADS, 2, PEER_QUERY_DIM // 2)
    s1 = jnp.einsum('thc,kc->thk', q[:, :, 0], keys1).astype(jnp.float32)
    s2 = jnp.einsum('thc,kc->thk', q[:, :, 1], keys2).astype(jnp.float32)
    v1, i1 = lax.top_k(s1, PEER_TOPK)
    v2, i2 = lax.top_k(s2, PEER_TOPK)
    cand = (v1[..., :, None] + v2[..., None, :]).reshape(T, PEER_HEADS, PEER_TOPK * PEER_TOPK)
    cand_idx = (i1[..., :, None] * PEER_N_KEYS + i2[..., None, :]).reshape(T, PEER_HEADS, PEER_TOPK * PEER_TOPK)
    top_s, pos = lax.top_k(cand, PEER_TOPK)
    experts = jnp.take_along_axis(cand_idx, pos, axis=-1)
    gates = jax.nn.softmax(top_s, axis=-1).astype(xn.dtype)
    nblk = T // PEER_TOKEN_BLOCK

    def block(args):
        xb, eb, gb = args
        ue = u[eb]
        act = jax.nn.gelu(jnp.einsum('thkd,td->thk', ue, xb), approximate=False)
        return jnp.einsum('thk,thkd->td', gb * act, v[eb])

    out = lax.map(block, (xt.reshape(nblk, PEER_TOKEN_BLOCK, D),
                          experts.reshape(nblk, PEER_TOKEN_BLOCK, PEER_HEADS, PEER_TOPK),
                          gates.reshape(nblk, PEER_TOKEN_BLOCK, PEER_HEADS, PEER_TOPK)))
    return out.reshape(B, S, D)


def split_heads(t, n_heads):
    B, S, E = t.shape
    return t.reshape(B, S, n_heads, E // n_heads).transpose(0, 2, 1, 3)


def merge_heads(t):
    B, H, S, e = t.shape
    return t.transpose(0, 2, 1, 3).reshape(B, S, H * e)


def setup_inputs(seed: int = 0) -> dict:
    key = jax.random.key(seed)
    ks = jax.random.split(key, 16)
    f32 = jnp.float32
    nrm = lambda k, shape, scale: jax.random.normal(k, shape, f32) * scale
    return {
        "x": nrm(ks[0], (BATCH, SEQ, D_MODEL), 1.0),
        "ln1_g": 1.0 + nrm(ks[1], (DEPTH, D_MODEL), 0.02),
        "w_in": nrm(ks[2], (DEPTH, D_MODEL, D_IN), D_MODEL ** -0.5),
        "rel_bias": nrm(ks[3], (N_BUCKETS, N_ATTN_HEADS), 0.1),
        "gla_w_gate2": nrm(ks[4], (DEPTH, GLA_GATE_RANK, D_GLA_K), GLA_GATE_RANK ** -0.5),
        "gla_b_gate": nrm(ks[5], (DEPTH, D_GLA_K), 0.1),
        "gla_norm_g": 1.0 + nrm(ks[6], (DEPTH, N_GLA_HEADS, GLA_DV), 0.02),
        "w_out": nrm(ks[7], (DEPTH, D_MIX, D_MODEL), D_MIX ** -0.5),
        "ln2_g": 1.0 + nrm(ks[8], (DEPTH, D_MODEL), 0.02),
        "peer_w_query": nrm(ks[9], (DEPTH, D_MODEL, PEER_HEADS * PEER_QUERY_DIM), D_MODEL ** -0.5),
        "peer_keys1": nrm(ks[10], (DEPTH, PEER_N_KEYS, PEER_QUERY_DIM // 2), (PEER_QUERY_DIM // 2) ** -0.5),
        "peer_keys2": nrm(ks[11], (DEPTH, PEER_N_KEYS, PEER_QUERY_DIM // 2), (PEER_QUERY_DIM // 2) ** -0.5),
        "peer_u": nrm(ks[12], (DEPTH, PEER_N_EXPERTS, D_MODEL), D_MODEL ** -0.5),
        "peer_v": nrm(ks[13], (DEPTH, PEER_N_EXPERTS, D_MODEL), PEER_HEADS ** -0.5),
        "ln_f_g": 1.0 + nrm(ks[14], (D_MODEL,), 0.02),
    }


def reference(x, ln1_g, w_in, rel_bias, gla_w_gate2, gla_b_gate, gla_norm_g, w_out,
              ln2_g, peer_w_query, peer_keys1, peer_keys2, peer_u, peer_v, ln_f_g):
    h = x
    for l in range(DEPTH):
        xn = rms_norm(h, ln1_g[l])
        proj = xn @ w_in[l]
        bounds = []
        acc = 0
        for w in IN_SPLITS[:-1]:
            acc += w
            bounds.append(acc)
        aq, ak, av, gq, gk, gv, ga, gr = jnp.split(proj, bounds, axis=-1)
        attn_o = dilated_attention(split_heads(aq, N_ATTN_HEADS), split_heads(ak, N_ATTN_HEADS),
                                   split_heads(av, N_ATTN_HEADS), rel_bias)
        attn_o = merge_heads(attn_o).astype(h.dtype)
        gate_logit = (ga @ gla_w_gate2[l] + gla_b_gate[l]).astype(jnp.float32)
        log_a = jax.nn.log_sigmoid(gate_logit) / GLA_GATE_TEMP
        gla_o = gla_chunked(split_heads(gq, N_GLA_HEADS).astype(jnp.float32) * (GLA_DK ** -0.5),
                            split_heads(gk, N_GLA_HEADS).astype(jnp.float32),
                            split_heads(gv, N_GLA_HEADS).astype(jnp.float32),
                            split_heads(log_a, N_GLA_HEADS))
        gla_o = rms_norm(gla_o, gla_norm_g[l][:, None, :])
        gla_o = (merge_heads(gla_o) * jax.nn.silu(gr.astype(jnp.float32))).astype(h.dtype)
        mix = jnp.concatenate([attn_o, gla_o], axis=-1) @ w_out[l]
        h = h + mix
        hn = rms_norm(h, ln2_g[l])
        h = h + peer_ffn(hn, peer_w_query[l], peer_keys1[l], peer_keys2[l], peer_u[l], peer_v[l])
    return rms_norm(h, ln_f_g)
```

```python
import functools
import math

import numpy as np
import jax
import jax.numpy as jnp
from jax import lax
from jax.experimental import pallas as pl
from jax.experimental.pallas import tpu as pltpu

F32 = jnp.float32
BF16 = jnp.bfloat16

D_MODEL = 2048
SEQ = 2048
EPS = 1e-6
D_ATTN = D_MODEL // 2
ATTN_HEAD_DIM = 64
N_ATTN_HEADS = D_ATTN // ATTN_HEAD_DIM
DILATED_CONFIGS = ((128, 1), (512, 4), (2048, 16))
N_BUCKETS = 32
MAX_EXACT = N_BUCKETS // 2
MAX_DISTANCE = 2048
N_GLA_HEADS = 4
D_GLA_V = D_MODEL // 2
D_GLA_K = D_GLA_V // 2
GLA_DK = D_GLA_K // N_GLA_HEADS
GLA_DV = D_GLA_V // N_GLA_HEADS
GLA_GATE_RANK = 16
GLA_GATE_TEMP = 16.0
GLA_CHUNK = 64
D_MIX = D_ATTN + D_GLA_V
D_IN = D_ATTN * 3 + D_GLA_K * 2 + D_GLA_V * 2 + GLA_GATE_RANK
PEER_HEADS = 8
PEER_N_KEYS = 128
PEER_N_EXPERTS = PEER_N_KEYS * PEER_N_KEYS
PEER_TOPK = 16
PEER_QUERY_DIM = 256

LANES = 128
WIN = 128
HEAD_PAIRS = N_ATTN_HEADS // 2

COL_AQ = 0
COL_AK = COL_AQ + D_ATTN
COL_AV = COL_AK + D_ATTN
COL_GQ = COL_AV + D_ATTN
COL_GK = COL_GQ + D_GLA_K
COL_GV = COL_GK + D_GLA_K
COL_GR = COL_GV + D_GLA_V
COL_GA = COL_GR + D_GLA_V
IN_PROJ_TN = 1280
D_IN_PAD = 6400
assert COL_GA + LANES <= D_IN_PAD and D_IN_PAD % IN_PROJ_TN == 0

VMEM_LIMIT = 56 * 1024 * 1024


def _cparams(sem):
    return pltpu.CompilerParams(dimension_semantics=sem, vmem_limit_bytes=VMEM_LIMIT)


def _in_proj_kernel(x_ref, g_ref, w_ref, o_ref, xn_ref):
    @pl.when(pl.program_id(1) == 0)
    def _():
        x = x_ref[...]
        ms = jnp.mean(x * x, axis=-1, keepdims=True)
        xn_ref[...] = (x * lax.rsqrt(ms + EPS) * g_ref[...]).astype(BF16)

    o_ref[...] = jnp.dot(xn_ref[...], w_ref[...], preferred_element_type=F32)


def _in_proj(x2d, g, w):
    T = x2d.shape[0]
    tm, tn = 512, IN_PROJ_TN
    return pl.pallas_call(
        _in_proj_kernel,
        grid=(T // tm, D_IN_PAD // tn),
        in_specs=[
            pl.BlockSpec((tm, D_MODEL), lambda i, j: (i, 0)),
            pl.BlockSpec((1, D_MODEL), lambda i, j: (0, 0)),
            pl.BlockSpec((D_MODEL, tn), lambda i, j: (0, j)),
        ],
        out_specs=pl.BlockSpec((tm, tn), lambda i, j: (i, j)),
        out_shape=jax.ShapeDtypeStruct((T, D_IN_PAD), F32),
        scratch_shapes=[pltpu.VMEM((tm, D_MODEL), BF16)],
        compiler_params=_cparams(("parallel", "arbitrary")),
        name="in_proj",
    )(x2d, g, w)


def _attn_kernel(q_ref, k_ref, v_ref, bias_ref, o_ref,
                 qs, kp, vp, oc, lc, ob0, ob1, ob2, lb0, lb1, lb2):
    scale = ATTN_HEAD_DIM ** -0.5
    lane = lax.broadcasted_iota(jnp.int32, (1, LANES), 1)
    head_a = lane < ATTN_HEAD_DIM
    col = lax.broadcasted_iota(jnp.int32, (WIN, 2 * WIN), 1)
    obs, lbs = (ob0, ob1, ob2), (lb0, lb1, lb2)

    kp[0:WIN, :] = jnp.zeros((WIN, LANES), BF16)
    vp[0:WIN, :] = jnp.zeros((WIN, LANES), BF16)

    for g, (_, d) in enumerate(DILATED_CONFIGS):
        L = SEQ // d
        nb = L // WIN
        for r in range(d):
            if d == 1:
                qc, kc, vc = q_ref[...], k_ref[...], v_ref[...]
            else:
                qc = q_ref[pl.ds(r, L, stride=d), :]
                kc = k_ref[pl.ds(r, L, stride=d), :]
                vc = v_ref[pl.ds(r, L, stride=d), :]
            qs[0:L, :] = (qc * scale).astype(BF16)
            kp[WIN:WIN + L, :] = kc.astype(BF16)
            vp[WIN:WIN + L, :] = vc.astype(BF16)

            def blk_body(blk, carry, g=g):
                off = pl.multiple_of(blk * WIN, WIN)
                qb = qs[pl.ds(off, WIN), :]
                kb = kp[pl.ds(off, 2 * WIN), :]
                vb = vp[pl.ds(off, 2 * WIN), :]
                first_valid_col = jnp.where(blk > 0, 0, WIN)
                outs = []
                for head in range(2):
                    hm = head_a if head == 0 else jnp.logical_not(head_a)
                    qh = jnp.where(hm, qb, jnp.zeros_like(qb))
                    s = lax.dot_general(qh, kb, (((1,), (1,)), ((), ())),
                                        preferred_element_type=F32)
                    s = s + bias_ref[g, head]
                    s = jnp.where(col >= first_valid_col, s, -jnp.inf)
                    m = jnp.max(s, axis=-1, keepdims=True)
                    p = jnp.exp(s - m)
                    den = jnp.sum(p, axis=-1, keepdims=True)
                    o = jnp.dot(p.astype(BF16), vb, preferred_element_type=F32)
                    outs.append((o / den, m + jnp.log(den)))
                oc[pl.ds(off, WIN), :] = jnp.where(head_a, outs[0][0], outs[1][0])
                lc[pl.ds(off, WIN), :] = jnp.where(head_a, outs[0][1], outs[1][1])
                return carry

            lax.fori_loop(0, nb, blk_body, 0)
            if d == 1:
                obs[g][...] = oc[...]
                lbs[g][...] = lc[...]
            else:
                obs[g][pl.ds(r, L, stride=d), :] = oc[0:L, :]
                lbs[g][pl.ds(r, L, stride=d), :] = lc[0:L, :]

    l0, l1, l2 = lb0[...], lb1[...], lb2[...]
    m = jnp.maximum(jnp.maximum(l0, l1), l2)
    e0, e1, e2 = jnp.exp(l0 - m), jnp.exp(l1 - m), jnp.exp(l2 - m)
    o_ref[...] = (e0 * ob0[...] + e1 * ob1[...] + e2 * ob2[...]) / (e0 + e1 + e2)


def _attention(proj, bias, batch):
    T = proj.shape[0]
    blk = (SEQ, LANES)
    nq, nk, nv = COL_AQ // LANES, COL_AK // LANES, COL_AV // LANES
    return pl.pallas_call(
        _attn_kernel,
        grid=(batch, HEAD_PAIRS),
        in_specs=[
            pl.BlockSpec(blk, lambda b, p: (b, nq + p)),
            pl.BlockSpec(blk, lambda b, p: (b, nk + p)),
            pl.BlockSpec(blk, lambda b, p: (b, nv + p)),
            pl.BlockSpec((3, 2, WIN, 2 * WIN), lambda b, p: (0, p, 0, 0)),
        ],
        out_specs=pl.BlockSpec(blk, lambda b, p: (b, p)),
        out_shape=jax.ShapeDtypeStruct((T, D_ATTN), F32),
        scratch_shapes=[
            pltpu.VMEM((SEQ, LANES), BF16),
            pltpu.VMEM((SEQ + WIN, LANES), BF16),
            pltpu.VMEM((SEQ + WIN, LANES), BF16),
            pltpu.VMEM((SEQ, LANES), F32),
            pltpu.VMEM((SEQ, LANES), F32),
        ] + [pltpu.VMEM((SEQ, LANES), F32)] * 6,
        compiler_params=_cparams(("parallel", "parallel")),
        name="dilated_attn",
    )(proj, proj, proj, bias)


def _attn_bias_tables(rel_bias):
    a = np.arange(WIN)[:, None]
    c = np.arange(2 * WIN)[None, :]
    delta = WIN + a - c
    valid = (delta >= 0) & (delta <= WIN)
    tables = []
    for _, d in DILATED_CONFIGS:
        dist = jnp.asarray(np.maximum(delta, 0) * d, jnp.int32)
        nf = jnp.maximum(dist, 1).astype(F32)
        large = MAX_EXACT + (jnp.log(nf / MAX_EXACT) / math.log(MAX_DISTANCE / MAX_EXACT)
                             * (N_BUCKETS - MAX_EXACT)).astype(jnp.int32)
        large = jnp.minimum(large, N_BUCKETS - 1)
        bucket = jnp.where(dist < MAX_EXACT, dist, large)
        b = rel_bias[bucket].transpose(2, 0, 1).astype(F32)
        tables.append(jnp.where(jnp.asarray(valid)[None], b, -jnp.inf))
    return jnp.stack(tables, 0)


def _gla_kernel(q_ref, k_ref, v_ref, r_ref, a_ref, w2_ref, bg_ref, gn_ref, o_ref,
                la_scr, st_scr):
    C = GLA_CHUNK
    hi = lax.Precision.HIGHEST
    logit = jnp.dot(a_ref[...], w2_ref[...], precision=hi,
                    preferred_element_type=F32) + bg_ref[...]
    log_sig = jnp.minimum(logit, 0.0) - jnp.log1p(jnp.exp(-jnp.abs(logit)))
    la_scr[...] = log_sig / GLA_GATE_TEMP
    st_scr[...] = jnp.zeros_like(st_scr)
    row = lax.broadcasted_iota(jnp.int32, (C, C), 0)
    colc = lax.broadcasted_iota(jnp.int32, (C, C), 1)
    causal = row >= colc
    tri = causal.astype(F32)
    gn = gn_ref[...]
    scale = GLA_DK ** -0.5
    nt = (((1,), (1,)), ((), ()))

    def body(n, carry):
        off = pl.multiple_of(n * C, C)
        la = la_scr[pl.ds(off, C), :]
        b = jnp.dot(tri, la, precision=hi, preferred_element_type=F32)
        b_ref = b[C // 2 - 1:C // 2, :]
        b_last = b[C - 1:C, :]
        q = q_ref[pl.ds(off, C), :] * scale
        k = k_ref[pl.ds(off, C), :]
        v = v_ref[pl.ds(off, C), :].astype(BF16)
        qd = (q * jnp.exp(b - b_ref)).astype(BF16)
        kd = (k * jnp.exp(b_ref - b)).astype(BF16)
        attn = lax.dot_general(qd, kd, nt, preferred_element_type=F32)
        attn = jnp.where(causal, attn, 0.0)
        o = jnp.dot(attn.astype(BF16), v, preferred_element_type=F32)
        st = st_scr[...]
        qe = (q * jnp.exp(b)).astype(BF16)
        o = o + lax.dot_general(qe, st.astype(BF16), nt, preferred_element_type=F32)
        kk = (k * jnp.exp(b_last - b)).astype(BF16)
        kv_t = lax.dot_general(v, kk, (((0,), (0,)), ((), ())), preferred_element_type=F32)
        st_scr[...] = st * jnp.exp(b_last) + kv_t
        ms = jnp.mean(o * o, axis=-1, keepdims=True)
        y = o * lax.rsqrt(ms + EPS) * gn
        gr = r_ref[pl.ds(off, C), :]
        o_ref[pl.ds(off, C), :] = y * (gr / (1.0 + jnp.exp(-gr)))
        return carry

    lax.fori_loop(0, SEQ // C, body, 0)


def _gla(proj, w2p, bgate, gnorm, batch):
    T = proj.shape[0]
    nq, nk = COL_GQ // GLA_DK, COL_GK // GLA_DK
    nv, nr = COL_GV // GLA_DV, COL_GR // GLA_DV
    na = COL_GA // LANES
    return pl.pallas_call(
        _gla_kernel,
        grid=(batch, N_GLA_HEADS),
        in_specs=[
            pl.BlockSpec((SEQ, GLA_DK), lambda b, h: (b, nq + h)),
            pl.BlockSpec((SEQ, GLA_DK), lambda b, h: (b, nk + h)),
            pl.BlockSpec((SEQ, GLA_DV), lambda b, h: (b, nv + h)),
            pl.BlockSpec((SEQ, GLA_DV), lambda b, h: (b, nr + h)),
            pl.BlockSpec((SEQ, LANES), lambda b, h: (b, na)),
            pl.BlockSpec((LANES, GLA_DK), lambda b, h: (0, h)),
            pl.BlockSpec((1, GLA_DK), lambda b, h: (0, h)),
            pl.BlockSpec((None, 1, GLA_DV), lambda b, h: (h, 0, 0)),
        ],
        out_specs=pl.BlockSpec((SEQ, GLA_DV), lambda b, h: (b, h)),
        out_shape=jax.ShapeDtypeStruct((T, D_GLA_V), F32),
        scratch_shapes=[pltpu.VMEM((SEQ, GLA_DK), F32), pltpu.VMEM((GLA_DV, GLA_DK), F32)],
        compiler_params=_cparams(("parallel", "parallel")),
        name="gla",
    )(proj, proj, proj, proj, proj, w2p, bgate, gnorm)


def _out_proj_kernel(a_ref, g_ref, x_ref, wa_ref, wg_ref, ln_ref, h_ref, hn_ref):
    mix = jnp.dot(a_ref[...].astype(BF16), wa_ref[...], preferred_element_type=F32)
    mix = mix + jnp.dot(g_ref[...].astype(BF16), wg_ref[...], preferred_element_type=F32)
    h = x_ref[...] + mix
    h_ref[...] = h
    ms = jnp.mean(h * h, axis=-1, keepdims=True)
    hn_ref[...] = (h * lax.rsqrt(ms + EPS) * ln_ref[...]).astype(BF16)


def _out_proj(attn_o, gla_o, x2d, wa, wg, ln2):
    T = x2d.shape[0]
    tm = 512
    return pl.pallas_call(
        _out_proj_kernel,
        grid=(T // tm,),
        in_specs=[
            pl.BlockSpec((tm, D_ATTN), lambda i: (i, 0)),
            pl.BlockSpec((tm, D_GLA_V), lambda i: (i, 0)),
            pl.BlockSpec((tm, D_MODEL), lambda i: (i, 0)),
            pl.BlockSpec((D_ATTN, D_MODEL), lambda i: (0, 0)),
            pl.BlockSpec((D_GLA_V, D_MODEL), lambda i: (0, 0)),
            pl.BlockSpec((1, D_MODEL), lambda i: (0, 0)),
        ],
        out_specs=[
            pl.BlockSpec((tm, D_MODEL), lambda i: (i, 0)),
            pl.BlockSpec((tm, D_MODEL), lambda i: (i, 0)),
        ],
        out_shape=[jax.ShapeDtypeStruct((T, D_MODEL), F32),
                   jax.ShapeDtypeStruct((T, D_MODEL), BF16)],
        compiler_params=_cparams(("parallel",)),
        name="out_proj",
    )(attn_o, gla_o, x2d, wa, wg, ln2)


def _top_values(s, k):
    vals = []
    for _ in range(k):
        m = jnp.max(s, axis=0, keepdims=True)
        vals.append(m)
        s = jnp.where(s == m, -jnp.inf, s)
    return vals


def _kth_values_and_mass(s, k):
    rank = jnp.zeros_like(s[0:1])
    z = jnp.zeros_like(rank)
    ck = jnp.full_like(rank, -jnp.inf)
    ck1 = ck
    m0 = None
    for _ in range(k + 1):
        m = jnp.max(s, axis=0, keepdims=True)
        m0 = m if m0 is None else m0
        eq = s == m
        cnt = jnp.sum(jnp.where(eq, 1.0, 0.0), axis=0, keepdims=True)
        nxt = rank + cnt
        z = z + jnp.clip(k - rank, 0.0, cnt) * jnp.exp(m - m0)
        ck = jnp.where((rank < k) & (nxt >= k), m, ck)
        ck1 = jnp.where((rank < k + 1) & (nxt >= k + 1), m, ck1)
        rank = nxt
        s = jnp.where(eq, -jnp.inf, s)
    return ck, ck1, z


def _route_kernel(hn_ref, wq_ref, k1_ref, k2_ref, c_ref, w_ref, e2_ref, s2_ref):
    q = jnp.dot(hn_ref[...], wq_ref[...], preferred_element_type=F32)
    nt = (((1,), (1,)), ((), ()))
    half = PEER_QUERY_DIM // 2
    for h in range(PEER_HEADS):
        q1 = q[:, h * PEER_QUERY_DIM:h * PEER_QUERY_DIM + half].astype(BF16)
        q2 = q[:, h * PEER_QUERY_DIM + half:(h + 1) * PEER_QUERY_DIM].astype(BF16)
        s1 = lax.dot_general(k1_ref[...], q1, nt, preferred_element_type=F32)
        s2 = lax.dot_general(k2_ref[...], q2, nt, preferred_element_type=F32)
        v1 = _top_values(s1, PEER_TOPK + 1)
        v2 = _top_values(s2, PEER_TOPK + 1)
        neg = [jnp.full_like(v1[0], -jnp.inf)] * 7
        v2_all = jnp.concatenate(v2 + neg, axis=0)
        v2_top8 = jnp.concatenate(v2[:8], axis=0)
        v1_tail = jnp.concatenate(v1[8:] + neg, axis=0)
        cand = jnp.concatenate([v1[0] + v2_all] + [v1[i] + v2_top8 for i in range(1, 8)]
                               + [v1_tail + v2[0]], axis=0)
        ck, ck1, z = _kth_values_and_mass(cand, PEER_TOPK)
        thr = 0.5 * (ck + ck1)
        c_ref[h] = thr - s1
        w_ref[h] = jnp.exp(s1 - v1[0]) / z
        e2_ref[h] = jnp.exp(s2 - v2[0])
        s2_ref[h] = s2


def _route(hn, wq, k1, k2):
    T = hn.shape[0]
    tm = 256
    tbl = jax.ShapeDtypeStruct((PEER_HEADS, PEER_N_KEYS, T), F32)
    tbl_spec = pl.BlockSpec((PEER_HEADS, PEER_N_KEYS, tm), lambda i: (0, 0, i))
    return pl.pallas_call(
        _route_kernel,
        grid=(T // tm,),
        in_specs=[
            pl.BlockSpec((tm, D_MODEL), lambda i: (i, 0)),
            pl.BlockSpec((D_MODEL, PEER_HEADS * PEER_QUERY_DIM), lambda i: (0, 0)),
            pl.BlockSpec((PEER_N_KEYS, PEER_QUERY_DIM // 2), lambda i: (0, 0)),
            pl.BlockSpec((PEER_N_KEYS, PEER_QUERY_DIM // 2), lambda i: (0, 0)),
        ],
        out_specs=[tbl_spec] * 4,
        out_shape=[tbl] * 4,
        compiler_params=_cparams(("parallel",)),
        name="peer_route",
    )(hn, wq, k1, k2)


PEER_TM = 512
PEER_TE = 1024


def _peer_kernel(hn_ref, u_ref, v_ref, c_ref, w_ref, e2_ref, s2_ref, h_ref, lnf_ref,
                 out_ref, acc_ref):
    e = pl.program_id(1)
    na = PEER_TE // PEER_N_KEYS

    @pl.when(e == 0)
    def _():
        acc_ref[...] = jnp.zeros_like(acc_ref)

    act_t = lax.dot_general(u_ref[...], hn_ref[...], (((1,), (1,)), ((), ())),
                            preferred_element_type=F32)
    parts = []
    for ai in range(na):
        lane_parts = []
        for lc in range(PEER_TM // LANES):
            ls = slice(lc * LANES, (lc + 1) * LANES)
            g = jnp.zeros((PEER_N_KEYS, LANES), F32)
            for h in range(PEER_HEADS):
                c = c_ref[h, ai:ai + 1, ls]
                w = w_ref[h, ai:ai + 1, ls]
                g = g + jnp.where(s2_ref[h, :, ls] >= c, w * e2_ref[h, :, ls], 0.0)
            x = act_t[ai * PEER_N_KEYS:(ai + 1) * PEER_N_KEYS, ls]
            gelu = 0.5 * x * (1.0 + lax.erf(x * (2.0 ** -0.5)))
            lane_parts.append((g * gelu).astype(BF16))
        parts.append(jnp.concatenate(lane_parts, axis=1))
    p_t = jnp.concatenate(parts, axis=0)
    acc_ref[...] += lax.dot_general(p_t, v_ref[...], (((0,), (0,)), ((), ())),
                                    preferred_element_type=F32)

    @pl.when(e == pl.num_programs(1) - 1)
    def _():
        h = h_ref[...] + acc_ref[...]
        ms = jnp.mean(h * h, axis=-1, keepdims=True)
        out_ref[...] = h * lax.rsqrt(ms + EPS) * lnf_ref[...]


def _peer(hn, u, v, c, w, e2, s2, h, lnf):
    T = hn.shape[0]
    tm, te = PEER_TM, PEER_TE
    tbl_spec = pl.BlockSpec((PEER_HEADS, PEER_N_KEYS, tm), lambda i, e: (0, 0, i))
    row_spec = pl.BlockSpec((PEER_HEADS, te // PEER_N_KEYS, tm), lambda i, e: (0, e, i))
    return pl.pallas_call(
        _peer_kernel,
        grid=(T // tm, PEER_N_EXPERTS // te),
        in_specs=[
            pl.BlockSpec((tm, D_MODEL), lambda i, e: (i, 0)),
            pl.BlockSpec((te, D_MODEL), lambda i, e: (e, 0)),
            pl.BlockSpec((te, D_MODEL), lambda i, e: (e, 0)),
            row_spec, row_spec, tbl_spec, tbl_spec,
            pl.BlockSpec((tm, D_MODEL), lambda i, e: (i, 0)),
            pl.BlockSpec((1, D_MODEL), lambda i, e: (0, 0)),
        ],
        out_specs=pl.BlockSpec((tm, D_MODEL), lambda i, e: (i, 0)),
        out_shape=jax.ShapeDtypeStruct((T, D_MODEL), F32),
        scratch_shapes=[pltpu.VMEM((tm, D_MODEL), F32)],
        compiler_params=_cparams(("parallel", "arbitrary")),
        name="peer",
    )(hn, u, v, c, w, e2, s2, h, lnf)


def _prep_w_in(w_in):
    o_ga = D_ATTN * 3 + D_GLA_K * 2 + D_GLA_V
    main = jnp.concatenate([w_in[:, :o_ga], w_in[:, o_ga + GLA_GATE_RANK:]], axis=1)
    ga = w_in[:, o_ga:o_ga + GLA_GATE_RANK]
    pad = jnp.zeros((D_MODEL, D_IN_PAD - D_IN), w_in.dtype)
    return jnp.concatenate([main, ga, pad], axis=1).astype(BF16)


def kernel(x, ln1_g, w_in, rel_bias, gla_w_gate2, gla_b_gate, gla_norm_g, w_out, ln2_g,
           peer_w_query, peer_keys1, peer_keys2, peer_u, peer_v, ln_f_g):
    B, S, D = x.shape
    assert (S, D) == (SEQ, D_MODEL) and w_in.shape[0] == 1
    x2d = x.reshape(B * S, D)

    proj = _in_proj(x2d, ln1_g[0][None, :], _prep_w_in(w_in[0]))
    attn_o = _attention(proj, _attn_bias_tables(rel_bias), B)
    w2p = jnp.pad(gla_w_gate2[0], ((0, LANES - GLA_GATE_RANK), (0, 0)))
    gla_o = _gla(proj, w2p, gla_b_gate[0][None, :], gla_norm_g[0][:, None, :], B)
    w_o = w_out[0].astype(BF16)
    h, hn = _out_proj(attn_o, gla_o, x2d, w_o[:D_ATTN], w_o[D_ATTN:], ln2_g[0][None, :])
    c, w, e2, s2 = _route(hn, peer_w_query[0].astype(BF16),
                          peer_keys1[0].astype(BF16), peer_keys2[0].astype(BF16))
    out = _peer(hn, peer_u[0].astype(BF16), peer_v[0].astype(BF16), c, w, e2, s2,
                h, ln_f_g[None, :])
    return out.reshape(B, S, D)
```
